```python
import math
import jax, jax.numpy as jnp
from jax import lax
import numpy as np

D_MODEL = 1024
BATCH = 16
SEQ = 4096
DEPTH = 1

CONV_WIDTH = D_MODEL
CONV_K = 3
DA_HEADS = 8
DA_HEAD_DIM = 64
DA_VALUE_DIM = 2 * DA_HEAD_DIM
DA_WIDTH = DA_HEADS * DA_VALUE_DIM
Q_BLOCK = 128
COLS_CONV = 4 * CONV_WIDTH
COLS_ATTN = 3 * DA_WIDTH + DA_WIDTH
COLS_GATE = 2 * D_MODEL
IN_COLS = COLS_CONV + COLS_ATTN + COLS_GATE
NORM_EPS = 1e-6
HEAD_NORM_EPS = 1e-5

kernel_name = "hybrid_shortconv_diffattn_gated_merge"


def rms_norm(x, w, eps):
    xf = x.astype(jnp.float32)
    y = xf * lax.rsqrt(jnp.mean(xf * xf, axis=-1, keepdims=True) + eps)
    return (y * w.astype(jnp.float32)).astype(x.dtype)


def lambda_init(layer_idx):
    return 0.8 - 0.6 * math.exp(-0.3 * layer_idx)


def causal_short_conv(u, w):
    k = w.shape[0]
    s = u.shape[1]
    up = jnp.pad(u, ((0, 0), (k - 1, 0), (0, 0)))
    y = up[:, 0:s] * w[0]
    for j in range(1, k):
        y = y + up[:, j:j + s] * w[j]
    return y


def short_conv_branch(xa, c, b, z, conv_w, w_out):
    u = causal_short_conv(c * xa, conv_w)
    return (jax.nn.silu(z) * b * u) @ w_out


def diff_attention(q, k, v, lam):
    s_len = q.shape[1]
    scale = DA_HEAD_DIM ** -0.5
    qh = jnp.transpose(q, (0, 2, 3, 1, 4))
    kh = jnp.transpose(k, (0, 2, 3, 1, 4))
    vh = jnp.transpose(v, (0, 2, 1, 3))
    outs = []
    for i in range(s_len // Q_BLOCK):
        lo, hi = i * Q_BLOCK, (i + 1) * Q_BLOCK
        qb = qh[:, :, :, lo:hi]
        kb = kh[:, :, :, :hi]
        vb = vh[:, :, :hi]
        sc = jnp.einsum('bhcqd,bhckd->bhcqk', qb, kb).astype(jnp.float32) * scale
        mask = jnp.arange(hi)[None, :] <= jnp.arange(lo, hi)[:, None]
        sc = jnp.where(mask, sc, -jnp.inf)
        p = jax.nn.softmax(sc, axis=-1)
        attn = p[:, :, 0] - lam * p[:, :, 1]
        outs.append(jnp.einsum('bhqk,bhkd->bhqd', attn.astype(v.dtype), vb))
    o = jnp.concatenate(outs, axis=2)
    return jnp.transpose(o, (0, 2, 1, 3))


def diff_attention_branch(q, k, v, z, lq1, lk1, lq2, lk2, head_norm_w, w_out, layer_idx):
    bsz, s_len = q.shape[0], q.shape[1]
    lam_init = lambda_init(layer_idx)
    lam = (jnp.exp(jnp.sum(lq1.astype(jnp.float32) * lk1.astype(jnp.float32)))
           - jnp.exp(jnp.sum(lq2.astype(jnp.float32) * lk2.astype(jnp.float32)))
           + lam_init)
    q = q.reshape(bsz, s_len, DA_HEADS, 2, DA_HEAD_DIM)
    k = k.reshape(bsz, s_len, DA_HEADS, 2, DA_HEAD_DIM)
    v = v.reshape(bsz, s_len, DA_HEADS, DA_VALUE_DIM)
    o = diff_attention(q, k, v, lam)
    o = rms_norm(o, head_norm_w, HEAD_NORM_EPS) * (1.0 - lam_init)
    o = o.reshape(bsz, s_len, DA_WIDTH)
    return (jax.nn.silu(z) * o) @ w_out


def setup_inputs(seed: int = 0) -> dict:
    key = jax.random.key(seed)
    ks = jax.random.split(key, 16)
    f32 = jnp.float32
    x = jax.random.normal(ks[0], (BATCH, SEQ, D_MODEL), f32)
    norm_w = 1.0 + 0.02 * jax.random.normal(ks[1], (DEPTH, D_MODEL), f32)
    w_in = jax.random.normal(ks[2], (DEPTH, D_MODEL, IN_COLS), f32) * D_MODEL ** -0.5
    conv_w = jax.random.normal(ks[3], (DEPTH, CONV_K, CONV_WIDTH), f32) * CONV_K ** -0.5
    w_out_a = jax.random.normal(ks[4], (DEPTH, CONV_WIDTH, D_MODEL), f32) * CONV_WIDTH ** -0.5
    lambda_q1 = 0.1 * jax.random.normal(ks[5], (DEPTH, DA_HEAD_DIM), f32)
    lambda_k1 = 0.1 * jax.random.normal(ks[6], (DEPTH, DA_HEAD_DIM), f32)
    lambda_q2 = 0.1 * jax.random.normal(ks[7], (DEPTH, DA_HEAD_DIM), f32)
    lambda_k2 = 0.1 * jax.random.normal(ks[8], (DEPTH, DA_HEAD_DIM), f32)
    head_norm_w = 1.0 + 0.02 * jax.random.normal(ks[9], (DEPTH, DA_VALUE_DIM), f32)
    w_out_b = jax.random.normal(ks[10], (DEPTH, DA_WIDTH, D_MODEL), f32) * DA_WIDTH ** -0.5
    b_gate = 0.02 * jax.random.normal(ks[11], (DEPTH, COLS_GATE), f32)
    w_o = jax.random.normal(ks[12], (DEPTH, D_MODEL, D_MODEL), f32) * D_MODEL ** -0.5
    final_norm_w = 1.0 + 0.02 * jax.random.normal(ks[13], (D_MODEL,), f32)
    return {"x": x, "norm_w": norm_w, "w_in": w_in, "conv_w": conv_w, "w_out_a": w_out_a,
            "lambda_q1": lambda_q1, "lambda_k1": lambda_k1, "lambda_q2": lambda_q2,
            "lambda_k2": lambda_k2, "head_norm_w": head_norm_w, "w_out_b": w_out_b,
            "b_gate": b_gate, "w_o": w_o, "final_norm_w": final_norm_w}


def reference(x, norm_w, w_in, conv_w, w_out_a, lambda_q1, lambda_k1, lambda_q2, lambda_k2,
              head_norm_w, w_out_b, b_gate, w_o, final_norm_w):
    c0 = 0
    c1 = c0 + COLS_CONV
    c2 = c1 + COLS_ATTN
    for layer in range(DEPTH):
        h = rms_norm(x, norm_w[layer], NORM_EPS)
        proj = h @ w_in[layer]
        xa, cg, bg, za = jnp.split(proj[..., c0:c1], 4, axis=-1)
        q, k, v, zb = jnp.split(proj[..., c1:c2], 4, axis=-1)
        gates = jax.nn.sigmoid(proj[..., c2:] + b_gate[layer])
        g_a, g_b = jnp.split(gates, 2, axis=-1)
        y_a = short_conv_branch(xa, cg, bg, za, conv_w[layer], w_out_a[layer])
        y_b = diff_attention_branch(q, k, v, zb, lambda_q1[layer], lambda_k1[layer],
                                    lambda_q2[layer], lambda_k2[layer], head_norm_w[layer],
                                    w_out_b[layer], layer)
        x = x + (g_a * y_a + g_b * y_b) @ w_o[layer]
    return rms_norm(x, final_norm_w, NORM_EPS)
```

```python
import functools
import math

import jax
import jax.numpy as jnp
from jax import lax
from jax.experimental import pallas as pl
from jax.experimental.pallas import tpu as pltpu

NORM_EPS = 1e-6
HEAD_NORM_EPS = 1e-5
CONV_K = 3
DA_HEADS = 8
DA_HEAD_DIM = 64
DA_VALUE_DIM = 2 * DA_HEAD_DIM

ROW_TILE = 512
CONV_CHUNK = 256
Q_TILE = 256
KV_TILE = 256
SUBLANES = 8
VMEM_LIMIT_BYTES = 56 * 1024 * 1024

_BF16 = jnp.bfloat16
_F32 = jnp.float32


def _sigmoid(z):
    return 1.0 / (1.0 + jnp.exp(-z))


def _dot(a, b):
    return jnp.dot(a, b, preferred_element_type=_F32)


def _dot_nt(a, b):
    return lax.dot_general(a, b, (((1,), (1,)), ((), ())), preferred_element_type=_F32)


def _dot_tn(a, b):
    return lax.dot_general(a, b, (((0,), (0,)), ((), ())), preferred_element_type=_F32)


def _proj_kernel(x_ref, nw_ref, wc_ref, wk_ref, wt_ref, wg_ref, bg_ref, cw_ref, woa_ref,
                 qt_ref, k_ref, vt_ref, zbt_ref, gb_ref, ma_ref, carry_ref):
    tm = x_ref.shape[1]
    d = x_ref.shape[2]
    width = woa_ref.shape[0]

    @pl.when(pl.program_id(1) == 0)
    def _():
        carry_ref[...] = jnp.zeros_like(carry_ref)

    xf = x_ref[0]
    ms = jnp.mean(xf * xf, axis=-1, keepdims=True)
    h = (xf * lax.rsqrt(ms + NORM_EPS) * nw_ref[...]).astype(_BF16)

    row8 = lax.broadcasted_iota(jnp.int32, (SUBLANES, CONV_CHUNK), 0)
    ya = jnp.zeros((tm, d), _F32)
    for j in range(width // CONV_CHUNK):
        c0 = j * CONV_CHUNK
        xa = _dot(h, wc_ref[:, 0 * width + c0:0 * width + c0 + CONV_CHUNK])
        cg = _dot(h, wc_ref[:, 1 * width + c0:1 * width + c0 + CONV_CHUNK])
        bgate = _dot(h, wc_ref[:, 2 * width + c0:2 * width + c0 + CONV_CHUNK])
        za = _dot(h, wc_ref[:, 3 * width + c0:3 * width + c0 + CONV_CHUNK])
        u = cg * xa
        prev = carry_ref[:, c0:c0 + CONV_CHUNK]
        carry_ref[:, c0:c0 + CONV_CHUNK] = u[tm - SUBLANES:, :]
        u1 = pltpu.roll(u, 1, 0)
        u2 = pltpu.roll(u, 2, 0)
        p1 = pltpu.roll(prev, 1, 0)
        p2 = pltpu.roll(prev, 2, 0)
        u1 = jnp.concatenate([jnp.where(row8 < 1, p1, u1[:SUBLANES]), u1[SUBLANES:]], axis=0)
        u2 = jnp.concatenate([jnp.where(row8 < 2, p2, u2[:SUBLANES]), u2[SUBLANES:]], axis=0)
        cw = cw_ref[:, c0:c0 + CONV_CHUNK]
        conv = u2 * cw[0:1] + u1 * cw[1:2] + u * cw[2:3]
        gated = (za * _sigmoid(za)) * bgate * conv
        ya = ya + _dot(gated.astype(_BF16), woa_ref[c0:c0 + CONV_CHUNK, :])

    k_ref[0] = _dot(h, wk_ref[...]).astype(_BF16)
    scale = DA_HEAD_DIM ** -0.5
    qt_ref[0] = (_dot_nt(wt_ref[0 * d:1 * d, :], h) * scale).astype(_BF16)
    vt_ref[0] = _dot_nt(wt_ref[1 * d:2 * d, :], h).astype(_BF16)
    zbt_ref[0] = _dot_nt(wt_ref[2 * d:3 * d, :], h).astype(_BF16)

    g = _sigmoid(_dot(h, wg_ref[...]) + bg_ref[...])
    ma_ref[0] = (g[:, :d] * ya).astype(_BF16)
    gb_ref[0] = g[:, d:].astype(_BF16)


def _proj_call(x, nw, wc, wk, wt, wg, bg, cw, woa):
    b, s, d = x.shape
    tm = ROW_TILE
    const = lambda shape: pl.BlockSpec(shape, lambda bi, si: (0,) * len(shape),
                                       pipeline_mode=pl.Buffered(1))
    row_blk = pl.BlockSpec((1, tm, d), lambda bi, si: (bi, si, 0))
    col_blk = pl.BlockSpec((1, d, tm), lambda bi, si: (bi, 0, si))
    nat = jax.ShapeDtypeStruct((b, s, d), _BF16)
    tr = jax.ShapeDtypeStruct((b, d, s), _BF16)
    return pl.pallas_call(
        _proj_kernel,
        grid=(b, s // tm),
        in_specs=[row_blk, const(nw.shape), const(wc.shape), const(wk.shape), const(wt.shape),
                  const(wg.shape), const(bg.shape), const(cw.shape), const(woa.shape)],
        out_specs=[col_blk, row_blk, col_blk, col_blk, row_blk, row_blk],
        out_shape=[tr, nat, tr, tr, nat, nat],
        scratch_shapes=[pltpu.VMEM((SUBLANES, d), _F32)],
        compiler_params=pltpu.CompilerParams(
            dimension_semantics=("arbitrary", "arbitrary"),
            vmem_limit_bytes=VMEM_LIMIT_BYTES),
        name="proj_conv_gates",
    )(x, nw, wc, wk, wt, wg, bg, cw, woa)


def _attn_kernel(lam_ref, hnw_ref, qt_ref, k_ref, vt_ref, zbt_ref, o_ref,
                 acc_ref, m_ref, l_ref, *, lam_init):
    s_len = k_ref.shape[1]
    tq, tk = Q_TILE, KV_TILE
    nq = s_len // tq

    lp = lam_ref[...]
    lam = (jnp.exp(jnp.sum(lp[0:1] * lp[1:2], axis=1, keepdims=True))
           - jnp.exp(jnp.sum(lp[2:3] * lp[3:4], axis=1, keepdims=True)) + lam_init)

    row_q = lax.broadcasted_iota(jnp.int32, (DA_VALUE_DIM, tq), 0)
    kv_idx = lax.broadcasted_iota(jnp.int32, (tk, 2 * tq), 0)
    col = lax.broadcasted_iota(jnp.int32, (tk, 2 * tq), 1)
    q_idx = jnp.where(col >= tq, col - tq, col)
    causal = kv_idx <= q_idx

    def step(q2, j, masked):
        k0 = pl.multiple_of(j * tk, tk)
        kb = k_ref[0, pl.ds(k0, tk), :]
        s = _dot(kb, q2)
        if masked:
            s = jnp.where(causal, s, -jnp.inf)
        m_old = m_ref[...]
        m_new = jnp.maximum(m_old, jnp.max(s, axis=0, keepdims=True))
        alpha = jnp.exp(m_old - m_new)
        p = jnp.exp(s - m_new)
        l_ref[...] = alpha * l_ref[...] + jnp.sum(p, axis=0, keepdims=True)
        m_ref[...] = m_new
        vb = vt_ref[0, :, pl.ds(k0, tk)]
        acc_ref[...] = acc_ref[...] * alpha + _dot(vb, p.astype(_BF16))

    def q_block(i, carry):
        q0 = pl.multiple_of(i * tq, tq)
        qt = qt_ref[0, :, pl.ds(q0, tq)]
        zero = jnp.zeros_like(qt)
        q2 = jnp.concatenate([jnp.where(row_q < DA_HEAD_DIM, qt, zero),
                              jnp.where(row_q >= DA_HEAD_DIM, qt, zero)], axis=1)
        m_ref[...] = jnp.full_like(m_ref, -jnp.inf)
        l_ref[...] = jnp.zeros_like(l_ref)
        acc_ref[...] = jnp.zeros_like(acc_ref)

        def kv_body(j, c):
            step(q2, j, masked=False)
            return c
        lax.fori_loop(0, i, kv_body, 0)
        step(q2, i, masked=True)

        inv = 1.0 / l_ref[...]
        acc = acc_ref[...]
        o = acc[:, :tq] * inv[:, :tq] - lam * (acc[:, tq:] * inv[:, tq:])
        ms = jnp.mean(o * o, axis=0, keepdims=True)
        on = (o * lax.rsqrt(ms + HEAD_NORM_EPS) * hnw_ref[...]) * (1.0 - lam_init)
        zb = zbt_ref[0, :, pl.ds(q0, tq)].astype(_F32)
        o_ref[0, :, pl.ds(q0, tq)] = ((zb * _sigmoid(zb)) * on).astype(_BF16)
        return carry

    lax.fori_loop(0, nq, q_block, 0)


def _attn_call(lam_params, hnw_b, qt, k, vt, zbt, lam_init):
    b, d, s = qt.shape
    hd = DA_VALUE_DIM
    col_blk = pl.BlockSpec((1, hd, s), lambda bi, hi: (bi, hi, 0))
    return pl.pallas_call(
        functools.partial(_attn_kernel, lam_init=lam_init),
        grid=(b, DA_HEADS),
        in_specs=[pl.BlockSpec(lam_params.shape, lambda bi, hi: (0, 0)),
                  pl.BlockSpec(hnw_b.shape, lambda bi, hi: (0, 0)),
                  col_blk,
                  pl.BlockSpec((1, s, hd), lambda bi, hi: (bi, 0, hi)),
                  col_blk, col_blk],
        out_specs=col_blk,
        out_shape=jax.ShapeDtypeStruct((b, d, s), _BF16),
        scratch_shapes=[pltpu.VMEM((hd, 2 * Q_TILE), _F32),
                        pltpu.VMEM((1, 2 * Q_TILE), _F32),
                        pltpu.VMEM((1, 2 * Q_TILE), _F32)],
        compiler_params=pltpu.CompilerParams(
            dimension_semantics=("arbitrary", "arbitrary"),
            vmem_limit_bytes=VMEM_LIMIT_BYTES),
        name="diff_attention",
    )(lam_params, hnw_b, qt, k, vt, zbt)


def _merge_kernel(x_ref, ot_ref, gb_ref, ma_ref, wob_ref, wo_ref, fw_ref, out_ref):
    yb = _dot_tn(ot_ref[0], wob_ref[...])
    z = gb_ref[0].astype(_F32) * yb + ma_ref[0].astype(_F32)
    xo = x_ref[0] + _dot(z.astype(_BF16), wo_ref[...])
    ms = jnp.mean(xo * xo, axis=-1, keepdims=True)
    out_ref[0] = xo * lax.rsqrt(ms + NORM_EPS) * fw_ref[...]


def _merge_call(x, ot, gb, ma, wob, wo, fw):
    b, s, d = x.shape
    tm = ROW_TILE
    const = lambda shape: pl.BlockSpec(shape, lambda bi, si: (0,) * len(shape),
                                       pipeline_mode=pl.Buffered(1))
    row_blk = pl.BlockSpec((1, tm, d), lambda bi, si: (bi, si, 0))
    col_blk = pl.BlockSpec((1, d, tm), lambda bi, si: (bi, 0, si))
    return pl.pallas_call(
        _merge_kernel,
        grid=(b, s // tm),
        in_specs=[row_blk, col_blk, row_blk, row_blk,
                  const(wob.shape), const(wo.shape), const(fw.shape)],
        out_specs=row_blk,
        out_shape=jax.ShapeDtypeStruct((b, s, d), x.dtype),
        compiler_params=pltpu.CompilerParams(
            dimension_semantics=("arbitrary", "arbitrary"),
            vmem_limit_bytes=VMEM_LIMIT_BYTES),
        name="merge_out",
    )(x, ot, gb, ma, wob, wo, fw)


def _lambda_init(layer_idx):
    return 0.8 - 0.6 * math.exp(-0.3 * layer_idx)


def kernel(x, norm_w, w_in, conv_w, w_out_a, lambda_q1, lambda_k1, lambda_q2, lambda_k2,
           head_norm_w, w_out_b, b_gate, w_o, final_norm_w):
    depth = norm_w.shape[0]
    assert depth == 1, "the merge kernel fuses the final RMSNorm, so only one layer is supported"
    d = x.shape[-1]
    width = conv_w.shape[-1]
    assert width == d and x.shape[1] % ROW_TILE == 0 and x.shape[1] % Q_TILE == 0
    c1 = 4 * width
    for layer in range(depth):
        lam_init = _lambda_init(layer)
        w = w_in[layer].astype(_BF16)
        wc = w[:, :c1]
        wq, wk, wv, wzb = (w[:, c1 + i * d:c1 + (i + 1) * d] for i in range(4))
        wt = jnp.concatenate([wq.T, wv.T, wzb.T], axis=0)
        wg = w[:, c1 + 4 * d:]
        qt, k, vt, zbt, gb, ma = _proj_call(
            x, norm_w[layer][None, :], wc, wk, wt, wg, b_gate[layer][None, :],
            conv_w[layer], w_out_a[layer].astype(_BF16))
        lam_params = jnp.stack([lambda_q1[layer], lambda_k1[layer],
                                lambda_q2[layer], lambda_k2[layer]], axis=0)
        hnw_b = jnp.broadcast_to(head_norm_w[layer][:, None], (DA_VALUE_DIM, Q_TILE))
        ot = _attn_call(lam_params, hnw_b, qt, k, vt, zbt, lam_init)
        x = _merge_call(x, ot, gb, ma, w_out_b[layer].astype(_BF16),
                        w_o[layer].astype(_BF16), final_norm_w[None, :])
    return x
```

```python
import functools
import math

import jax
import jax.numpy as jnp
from jax import lax
from jax.experimental import pallas as pl
from jax.experimental.pallas import tpu as pltpu

NORM_EPS = 1e-6
HEAD_NORM_EPS = 1e-5
CONV_K = 3
DA_HEADS = 8
DA_HEAD_DIM = 64
DA_VALUE_DIM = 2 * DA_HEAD_DIM

ROW_TILE = 512
CONV_CHUNK = 256
KV_TILE = 256
Q_TILE = 2 * KV_TILE
HEADS_PER_STEP = 1
SUBLANES = 8
VMEM_LIMIT_BYTES = 56 * 1024 * 1024

_BF16 = jnp.bfloat16
_F32 = jnp.float32


def _sigmoid(z):
    return 1.0 / (1.0 + jnp.exp(-z))


def _dot(a, b):
    return jnp.dot(a, b, preferred_element_type=_F32)


def _dot_nt(a, b):
    return lax.dot_general(a, b, (((1,), (1,)), ((), ())), preferred_element_type=_F32)


def _dot_tn(a, b):
    return lax.dot_general(a, b, (((0,), (0,)), ((), ())), preferred_element_type=_F32)


def _proj_kernel(x_ref, nw_ref, wc_ref, wk_ref, wt_ref, wg_ref, bg_ref, cw_ref, woa_ref,
                 qt_ref, k_ref, vt_ref, zbt_ref, gb_ref, ma_ref, carry_ref):
    tm = x_ref.shape[1]
    d = x_ref.shape[2]
    width = woa_ref.shape[0]

    @pl.when(pl.program_id(1) == 0)
    def _():
        carry_ref[...] = jnp.zeros_like(carry_ref)

    xf = x_ref[0]
    ms = jnp.mean(xf * xf, axis=-1, keepdims=True)
    h = (xf * lax.rsqrt(ms + NORM_EPS) * nw_ref[...]).astype(_BF16)

    row8 = lax.broadcasted_iota(jnp.int32, (SUBLANES, CONV_CHUNK), 0)
    ya = jnp.zeros((tm, d), _F32)
    for j in range(width // CONV_CHUNK):
        c0 = j * CONV_CHUNK
        xa = _dot(h, wc_ref[:, 0 * width + c0:0 * width + c0 + CONV_CHUNK])
        cg = _dot(h, wc_ref[:, 1 * width + c0:1 * width + c0 + CONV_CHUNK])
        bgate = _dot(h, wc_ref[:, 2 * width + c0:2 * width + c0 + CONV_CHUNK])
        za = _dot(h, wc_ref[:, 3 * width + c0:3 * width + c0 + CONV_CHUNK])
        u = cg * xa
        prev = carry_ref[:, c0:c0 + CONV_CHUNK]
        carry_ref[:, c0:c0 + CONV_CHUNK] = u[tm - SUBLANES:, :]
        u1 = pltpu.roll(u, 1, 0)
        u2 = pltpu.roll(u, 2, 0)
        p1 = pltpu.roll(prev, 1, 0)
        p2 = pltpu.roll(prev, 2, 0)
        u1 = jnp.concatenate([jnp.where(row8 < 1, p1, u1[:SUBLANES]), u1[SUBLANES:]], axis=0)
        u2 = jnp.concatenate([jnp.where(row8 < 2, p2, u2[:SUBLANES]), u2[SUBLANES:]], axis=0)
        cw = cw_ref[:, c0:c0 + CONV_CHUNK]
        conv = u2 * cw[0:1] + u1 * cw[1:2] + u * cw[2:3]
        gated = (za * _sigmoid(za)) * bgate * conv
        ya = ya + _dot(gated.astype(_BF16), woa_ref[c0:c0 + CONV_CHUNK, :])

    k_ref[0] = _dot(h, wk_ref[...]).astype(_BF16)
    scale = DA_HEAD_DIM ** -0.5 * math.log2(math.e)
    qt_ref[0] = (_dot_nt(wt_ref[0 * d:1 * d, :], h) * scale).astype(_BF16)
    vt_ref[0] = _dot_nt(wt_ref[1 * d:2 * d, :], h).astype(_BF16)
    zbt_ref[0] = _dot_nt(wt_ref[2 * d:3 * d, :], h).astype(_BF16)

    g = _sigmoid(_dot(h, wg_ref[...]) + bg_ref[...])
    ma_ref[0] = (g[:, :d] * ya).astype(_BF16)
    gb_ref[0] = g[:, d:].astype(_BF16)


def _proj_call(x, nw, wc, wk, wt, wg, bg, cw, woa):
    b, s, d = x.shape
    tm = ROW_TILE
    const = lambda shape: pl.BlockSpec(shape, lambda bi, si: (0,) * len(shape),
                                       pipeline_mode=pl.Buffered(1))
    row_blk = pl.BlockSpec((1, tm, d), lambda bi, si: (bi, si, 0))
    col_blk = pl.BlockSpec((1, d, tm), lambda bi, si: (bi, 0, si))
    nat = jax.ShapeDtypeStruct((b, s, d), _BF16)
    tr = jax.ShapeDtypeStruct((b, d, s), _BF16)
    return pl.pallas_call(
        _proj_kernel,
        grid=(b, s // tm),
        in_specs=[row_blk, const(nw.shape), const(wc.shape), const(wk.shape), const(wt.shape),
                  const(wg.shape), const(bg.shape), const(cw.shape), const(woa.shape)],
        out_specs=[col_blk, row_blk, col_blk, col_blk, row_blk, row_blk],
        out_shape=[tr, nat, tr, tr, nat, nat],
        scratch_shapes=[pltpu.VMEM((SUBLANES, d), _F32)],
        compiler_params=pltpu.CompilerParams(
            dimension_semantics=("arbitrary", "arbitrary"),
            vmem_limit_bytes=VMEM_LIMIT_BYTES),
        name="proj_conv_gates",
    )(x, nw, wc, wk, wt, wg, bg, cw, woa)


def _attn_kernel(lam_ref, hnw_ref, qt_ref, k_ref, vt_ref, zbt_ref, o_ref,
                 q2_ref, s_ref, acc_ref, m_ref, l_ref, *, lam_init):
    s_len = k_ref.shape[1]
    tq, tk = Q_TILE, KV_TILE
    hd = DA_VALUE_DIM
    hp = HEADS_PER_STEP
    nq = s_len // tq

    lp = lam_ref[...]
    lam = (jnp.exp(jnp.sum(lp[0:1] * lp[1:2], axis=1, keepdims=True))
           - jnp.exp(jnp.sum(lp[2:3] * lp[3:4], axis=1, keepdims=True)) + lam_init)

    row_q = lax.broadcasted_iota(jnp.int32, (hd, tq), 0)
    kv_idx = lax.broadcasted_iota(jnp.int32, (tk, 2 * tq), 0)
    col = lax.broadcasted_iota(jnp.int32, (tk, 2 * tq), 1)
    q_idx = jnp.where(col >= tq, col - tq, col)
    diag_masks = (kv_idx <= q_idx, kv_idx + tk <= q_idx)

    def scores(j, slot):
        k0 = pl.multiple_of(j * tk, tk)
        for hh in range(hp):
            kb = k_ref[0, pl.ds(k0, tk), hh * hd:(hh + 1) * hd]
            s_ref[slot, hh] = _dot(kb, q2_ref[hh])

    def softmax_pv(j, slot, mask):
        k0 = pl.multiple_of(j * tk, tk)
        ps, alphas = [], []
        for hh in range(hp):
            s = s_ref[slot, hh]
            if mask is not None:
                s = jnp.where(mask, s, -jnp.inf)
            m_old = m_ref[hh]
            m_new = jnp.maximum(m_old, jnp.max(s, axis=0, keepdims=True))
            alpha = jnp.exp2(m_old - m_new)
            p = jnp.exp2(s - m_new)
            l_ref[hh] = alpha * l_ref[hh] + jnp.sum(p, axis=0, keepdims=True)
            m_ref[hh] = m_new
            ps.append(p.astype(_BF16))
            alphas.append(alpha)
        for hh in range(hp):
            vb = vt_ref[0, hh * hd:(hh + 1) * hd, pl.ds(k0, tk)]
            acc_ref[hh] = acc_ref[hh] * alphas[hh] + _dot(vb, ps[hh])

    def q_block(i, carry):
        q0 = pl.multiple_of(i * tq, tq)
        for hh in range(hp):
            qt = qt_ref[0, hh * hd:(hh + 1) * hd, pl.ds(q0, tq)]
            zero = jnp.zeros_like(qt)
            q2_ref[hh] = jnp.concatenate([jnp.where(row_q < DA_HEAD_DIM, qt, zero),
                                          jnp.where(row_q >= DA_HEAD_DIM, qt, zero)], axis=1)
        m_ref[...] = jnp.full_like(m_ref, -jnp.inf)
        l_ref[...] = jnp.zeros_like(l_ref)
        acc_ref[...] = jnp.zeros_like(acc_ref)

        scores(0, 0)

        def pair_body(p, c):
            a = 2 * p
            scores(a + 1, 1)
            softmax_pv(a, 0, None)
            scores(a + 2, 0)
            softmax_pv(a + 1, 1, None)
            return c
        lax.fori_loop(0, i, pair_body, 0)
        a = 2 * i
        scores(a + 1, 1)
        softmax_pv(a, 0, diag_masks[0])
        softmax_pv(a + 1, 1, diag_masks[1])

        for hh in range(hp):
            rows = slice(hh * hd, (hh + 1) * hd)
            inv = 1.0 / l_ref[hh]
            acc = acc_ref[hh]
            o = acc[:, :tq] * inv[:, :tq] - lam * (acc[:, tq:] * inv[:, tq:])
            ms = jnp.mean(o * o, axis=0, keepdims=True)
            on = (o * lax.rsqrt(ms + HEAD_NORM_EPS) * hnw_ref[...]) * (1.0 - lam_init)
            zb = zbt_ref[0, rows, pl.ds(q0, tq)].astype(_F32)
            o_ref[0, rows, pl.ds(q0, tq)] = ((zb * _sigmoid(zb)) * on).astype(_BF16)
        return carry

    lax.fori_loop(0, nq, q_block, 0)


def _attn_call(lam_params, hnw_b, qt, k, vt, zbt, lam_init):
    b, d, s = qt.shape
    hp = HEADS_PER_STEP
    hd = DA_VALUE_DIM
    col_blk = pl.BlockSpec((1, hp * hd, s), lambda bi, gi: (bi, gi, 0))
    return pl.pallas_call(
        functools.partial(_attn_kernel, lam_init=lam_init),
        grid=(b, DA_HEADS // hp),
        in_specs=[pl.BlockSpec(lam_params.shape, lambda bi, gi: (0, 0)),
                  pl.BlockSpec(hnw_b.shape, lambda bi, gi: (0, 0)),
                  col_blk,
                  pl.BlockSpec((1, s, hp * hd), lambda bi, gi: (bi, 0, gi)),
                  col_blk, col_blk],
        out_specs=col_blk,
        out_shape=jax.ShapeDtypeStruct((b, d, s), _BF16),
        scratch_shapes=[pltpu.VMEM((hp, hd, 2 * Q_TILE), _BF16),
                        pltpu.VMEM((2, hp, KV_TILE, 2 * Q_TILE), _F32),
                        pltpu.VMEM((hp, hd, 2 * Q_TILE), _F32),
                        pltpu.VMEM((hp, 1, 2 * Q_TILE), _F32),
                        pltpu.VMEM((hp, 1, 2 * Q_TILE), _F32)],
        compiler_params=pltpu.CompilerParams(
            dimension_semantics=("arbitrary", "arbitrary"),
            vmem_limit_bytes=VMEM_LIMIT_BYTES),
        name="diff_attention",
    )(lam_params, hnw_b, qt, k, vt, zbt)


def _merge_kernel(x_ref, ot_ref, gb_ref, ma_ref, wob_ref, wo_ref, fw_ref, out_ref):
    yb = _dot_tn(ot_ref[0], wob_ref[...])
    z = gb_ref[0].astype(_F32) * yb + ma_ref[0].astype(_F32)
    xo = x_ref[0] + _dot(z.astype(_BF16), wo_ref[...])
    ms = jnp.mean(xo * xo, axis=-1, keepdims=True)
    out_ref[0] = xo * lax.rsqrt(ms + NORM_EPS) * fw_ref[...]


def _merge_call(x, ot, gb, ma, wob, wo, fw):
    b, s, d = x.shape
    tm = ROW_TILE
    const = lambda shape: pl.BlockSpec(shape, lambda bi, si: (0,) * len(shape),
                                       pipeline_mode=pl.Buffered(1))
    row_blk = pl.BlockSpec((1, tm, d), lambda bi, si: (bi, si, 0))
    col_blk = pl.BlockSpec((1, d, tm), lambda bi, si: (bi, 0, si))
    return pl.pallas_call(
        _merge_kernel,
        grid=(b, s // tm),
        in_specs=[row_blk, col_blk, row_blk, row_blk,
                  const(wob.shape), const(wo.shape), const(fw.shape)],
        out_specs=row_blk,
        out_shape=jax.ShapeDtypeStruct((b, s, d), x.dtype),
        compiler_params=pltpu.CompilerParams(
            dimension_semantics=("arbitrary", "arbitrary"),
            vmem_limit_bytes=VMEM_LIMIT_BYTES),
        name="merge_out",
    )(x, ot, gb, ma, wob, wo, fw)


def _lambda_init(layer_idx):
    return 0.8 - 0.6 * math.exp(-0.3 * layer_idx)


def kernel(x, norm_w, w_in, conv_w, w_out_a, lambda_q1, lambda_k1, lambda_q2, lambda_k2,
           head_norm_w, w_out_b, b_gate, w_o, final_norm_w):
    depth = norm_w.shape[0]
    assert depth == 1, "the merge kernel fuses the final RMSNorm, so only one layer is supported"
    d = x.shape[-1]
    width = conv_w.shape[-1]
    assert width == d and x.shape[1] % ROW_TILE == 0 and x.shape[1] % Q_TILE == 0
    c1 = 4 * width
    for layer in range(depth):
        lam_init = _lambda_init(layer)
        w = w_in[layer].astype(_BF16)
        wc = w[:, :c1]
        wq, wk, wv, wzb = (w[:, c1 + i * d:c1 + (i + 1) * d] for i in range(4))
        wt = jnp.concatenate([wq.T, wv.T, wzb.T], axis=0)
        wg = w[:, c1 + 4 * d:]
        qt, k, vt, zbt, gb, ma = _proj_call(
            x, norm_w[layer][None, :], wc, wk, wt, wg, b_gate[layer][None, :],
            conv_w[layer], w_out_a[layer].astype(_BF16))
        lam_params = jnp.stack([lambda_q1[layer], lambda_k1[layer],
                                lambda_q2[layer], lambda_k2[layer]], axis=0)
        hnw_b = jnp.broadcast_to(head_norm_w[layer][:, None], (DA_VALUE_DIM, Q_TILE))
        ot = _attn_call(lam_params, hnw_b, qt, k, vt, zbt, lam_init)
        x = _merge_call(x, ot, gb, ma, w_out_b[layer].astype(_BF16),
                        w_o[layer].astype(_BF16), final_norm_w[None, :])
    return x
```

```python
import functools
import math

import jax
import jax.numpy as jnp
from jax import lax
from jax.experimental import pallas as pl
from jax.experimental.pallas import tpu as pltpu

NORM_EPS = 1e-6
HEAD_NORM_EPS = 1e-5
CONV_K = 3
DA_HEADS = 8
DA_HEAD_DIM = 64
DA_VALUE_DIM = 2 * DA_HEAD_DIM

ROW_TILE = 512
CONV_CHUNK = 256
KV_TILE = 256
Q_TILE = 2 * KV_TILE
COL_CHUNK = KV_TILE
SUBLANES = 8
BF16_SUBLANES = 16
VMEM_LIMIT_BYTES = 56 * 1024 * 1024

_BF16 = jnp.bfloat16
_F32 = jnp.float32


def _sigmoid(z):
    return 1.0 / (1.0 + jnp.exp(-z))


def _dot(a, b):
    return jnp.dot(a, b, preferred_element_type=_F32)


def _dot_nt(a, b):
    return lax.dot_general(a, b, (((1,), (1,)), ((), ())), preferred_element_type=_F32)


def _dot_tn(a, b):
    return lax.dot_general(a, b, (((0,), (0,)), ((), ())), preferred_element_type=_F32)


def _proj_kernel(x_ref, nw_ref, wc_ref, wk_ref, wt_ref, wg_ref, bg_ref, cw_ref, woa_ref,
                 qt_ref, k_ref, vt_ref, zbt_ref, gb_ref, ma_ref, carry_ref):
    tm = x_ref.shape[1]
    d = x_ref.shape[2]
    width = woa_ref.shape[0]

    @pl.when(pl.program_id(1) == 0)
    def _():
        carry_ref[...] = jnp.zeros_like(carry_ref)

    xf = x_ref[0]
    ms = jnp.mean(xf * xf, axis=-1, keepdims=True)
    h = (xf * lax.rsqrt(ms + NORM_EPS) * nw_ref[...]).astype(_BF16)

    row8 = lax.broadcasted_iota(jnp.int32, (SUBLANES, CONV_CHUNK), 0)
    ya = jnp.zeros((tm, d), _F32)
    for j in range(width // CONV_CHUNK):
        c0 = j * CONV_CHUNK
        xa = _dot(h, wc_ref[:, 0 * width + c0:0 * width + c0 + CONV_CHUNK])
        cg = _dot(h, wc_ref[:, 1 * width + c0:1 * width + c0 + CONV_CHUNK])
        bgate = _dot(h, wc_ref[:, 2 * width + c0:2 * width + c0 + CONV_CHUNK])
        za = _dot(h, wc_ref[:, 3 * width + c0:3 * width + c0 + CONV_CHUNK])
        u = cg * xa
        prev = carry_ref[:, c0:c0 + CONV_CHUNK]
        carry_ref[:, c0:c0 + CONV_CHUNK] = u[tm - SUBLANES:, :]
        u1 = pltpu.roll(u, 1, 0)
        u2 = pltpu.roll(u, 2, 0)
        p1 = pltpu.roll(prev, 1, 0)
        p2 = pltpu.roll(prev, 2, 0)
        u1 = jnp.concatenate([jnp.where(row8 < 1, p1, u1[:SUBLANES]), u1[SUBLANES:]], axis=0)
        u2 = jnp.concatenate([jnp.where(row8 < 2, p2, u2[:SUBLANES]), u2[SUBLANES:]], axis=0)
        cw = cw_ref[:, c0:c0 + CONV_CHUNK]
        conv = u2 * cw[0:1] + u1 * cw[1:2] + u * cw[2:3]
        gated = (za * _sigmoid(za)) * bgate * conv
        ya = ya + _dot(gated.astype(_BF16), woa_ref[c0:c0 + CONV_CHUNK, :])

    k_ref[0] = _dot(h, wk_ref[...]).astype(_BF16)
    scale = DA_HEAD_DIM ** -0.5 * math.log2(math.e)
    qt_ref[0] = (_dot_nt(wt_ref[0 * d:1 * d, :], h) * scale).astype(_BF16)
    vt_ref[0] = _dot_nt(wt_ref[1 * d:2 * d, :], h).astype(_BF16)
    zbt_ref[0] = _dot_nt(wt_ref[2 * d:3 * d, :], h).astype(_BF16)

    g = _sigmoid(_dot(h, wg_ref[...]) + bg_ref[...])
    ma_ref[0] = (g[:, :d] * ya).astype(_BF16)
    gb_ref[0] = g[:, d:].astype(_BF16)


def _proj_call(x, nw, wc, wk, wt, wg, bg, cw, woa):
    b, s, d = x.shape
    tm = ROW_TILE
    const = lambda shape: pl.BlockSpec(shape, lambda bi, si: (0,) * len(shape),
                                       pipeline_mode=pl.Buffered(1))
    row_blk = pl.BlockSpec((1, tm, d), lambda bi, si: (bi, si, 0))
    col_blk = pl.BlockSpec((1, d, tm), lambda bi, si: (bi, 0, si))
    nat = jax.ShapeDtypeStruct((b, s, d), _BF16)
    tr = jax.ShapeDtypeStruct((b, d, s), _BF16)
    return pl.pallas_call(
        _proj_kernel,
        grid=(b, s // tm),
        in_specs=[row_blk, const(nw.shape), const(wc.shape), const(wk.shape), const(wt.shape),
                  const(wg.shape), const(bg.shape), const(cw.shape), const(woa.shape)],
        out_specs=[col_blk, row_blk, col_blk, col_blk, row_blk, row_blk],
        out_shape=[tr, nat, tr, tr, nat, nat],
        scratch_shapes=[pltpu.VMEM((SUBLANES, d), _F32)],
        compiler_params=pltpu.CompilerParams(
            dimension_semantics=("arbitrary", "arbitrary"),
            vmem_limit_bytes=VMEM_LIMIT_BYTES),
        name="proj_conv_gates",
    )(x, nw, wc, wk, wt, wg, bg, cw, woa)


def _attn_kernel(lam_ref, hnw_ref, qt_ref, k_ref, vt_ref, zbt_ref, o_ref,
                 q2_ref, s_ref, tri_ref, acc_ref, m_ref, *, lam_init):
    s_len = k_ref.shape[1]
    tq, tk, cw = Q_TILE, KV_TILE, COL_CHUNK
    hd = DA_VALUE_DIM
    nq = s_len // tq
    n_chunks = 2 * tq // cw
    chunks_per_half = tq // cw

    lp = lam_ref[...]
    lam = (jnp.exp(jnp.sum(lp[0:1] * lp[1:2], axis=1, keepdims=True))
           - jnp.exp(jnp.sum(lp[2:3] * lp[3:4], axis=1, keepdims=True)) + lam_init)

    kv_idx = lax.broadcasted_iota(jnp.int32, (tk, cw), 0)
    q_idx = lax.broadcasted_iota(jnp.int32, (tk, cw), 1)
    tri_ref[...] = jnp.where(kv_idx <= q_idx, 0.0, -jnp.inf).astype(_BF16)
    ones_rows = jnp.ones((BF16_SUBLANES, tk), _BF16)
    row_q = lax.broadcasted_iota(jnp.int32, (hd, tq), 0)

    def build_q2(i):
        qt = qt_ref[0, :, i * tq:(i + 1) * tq]
        zero = jnp.zeros_like(qt)
        q2_ref[i % 2] = jnp.concatenate([jnp.where(row_q < DA_HEAD_DIM, qt, zero),
                                         jnp.where(row_q >= DA_HEAD_DIM, qt, zero)], axis=1)

    def live_chunks(i, j):
        kv0 = j * tk
        return [c for c in range(n_chunks) if kv0 < i * tq + (c % chunks_per_half + 1) * cw]

    def scores(i, j):
        kb = k_ref[0, j * tk:(j + 1) * tk, :]
        for c in live_chunks(i, j):
            cols = slice(c * cw, (c + 1) * cw)
            s_ref[j % 2, :, cols] = _dot(kb, q2_ref[i % 2, :, cols]).astype(_BF16)

    def softmax_pv(i, j):
        vb = jnp.concatenate([vt_ref[0, :, j * tk:(j + 1) * tk], ones_rows], axis=0)
        for c in live_chunks(i, j):
            cols = slice(c * cw, (c + 1) * cw)
            s = s_ref[j % 2, :, cols]
            if j * tk == i * tq + (c % chunks_per_half) * cw:
                s = s + tri_ref[...]
            m_old = m_ref[:, cols]
            m_new = jnp.maximum(m_old, jnp.max(s, axis=0, keepdims=True).astype(_F32))
            m_ref[:, cols] = m_new
            p = jnp.exp2(s - m_new.astype(_BF16))
            alpha = jnp.exp2(m_old - m_new)
            acc_ref[:, cols] = acc_ref[:, cols] * alpha + _dot(vb, p)

    def finalize(i):
        inv = 1.0 / acc_ref[hd:hd + 1, :]
        acc = acc_ref[:hd, :]
        o = acc[:, :tq] * inv[:, :tq] - lam * (acc[:, tq:] * inv[:, tq:])
        ms = jnp.mean(o * o, axis=0, keepdims=True)
        on = (o * lax.rsqrt(ms + HEAD_NORM_EPS) * hnw_ref[...]) * (1.0 - lam_init)
        zb = zbt_ref[0, :, i * tq:(i + 1) * tq].astype(_F32)
        o_ref[0, :, i * tq:(i + 1) * tq] = ((zb * _sigmoid(zb)) * on).astype(_BF16)

    build_q2(0)
    scores(0, 0)
    for i in range(nq):
        n_blocks = (i + 1) * tq // tk
        m_ref[...] = jnp.full_like(m_ref, -jnp.inf)
        acc_ref[...] = jnp.zeros_like(acc_ref)
        for j in range(n_blocks):
            if j + 1 < n_blocks:
                scores(i, j + 1)
            elif i + 1 < nq:
                build_q2(i + 1)
                scores(i + 1, 0)
            softmax_pv(i, j)
        finalize(i)


def _attn_call(lam_params, hnw_b, qt, k, vt, zbt, lam_init):
    b, d, s = qt.shape
    hd = DA_VALUE_DIM
    assert COL_CHUNK == KV_TILE and Q_TILE % COL_CHUNK == 0 and s % Q_TILE == 0
    col_blk = pl.BlockSpec((1, hd, s), lambda bi, hi: (bi, hi, 0))
    return pl.pallas_call(
        functools.partial(_attn_kernel, lam_init=lam_init),
        grid=(b, DA_HEADS),
        in_specs=[pl.BlockSpec(lam_params.shape, lambda bi, hi: (0, 0)),
                  pl.BlockSpec(hnw_b.shape, lambda bi, hi: (0, 0)),
                  col_blk,
                  pl.BlockSpec((1, s, hd), lambda bi, hi: (bi, 0, hi)),
                  col_blk, col_blk],
        out_specs=col_blk,
        out_shape=jax.ShapeDtypeStruct((b, d, s), _BF16),
        scratch_shapes=[pltpu.VMEM((2, hd, 2 * Q_TILE), _BF16),
                        pltpu.VMEM((2, KV_TILE, 2 * Q_TILE), _BF16),
                        pltpu.VMEM((KV_TILE, COL_CHUNK), _BF16),
                        pltpu.VMEM((hd + BF16_SUBLANES, 2 * Q_TILE), _F32),
                        pltpu.VMEM((1, 2 * Q_TILE), _F32)],
        compiler_params=pltpu.CompilerParams(
            dimension_semantics=("arbitrary", "arbitrary"),
            vmem_limit_bytes=VMEM_LIMIT_BYTES),
        name="diff_attention",
    )(lam_params, hnw_b, qt, k, vt, zbt)


def _merge_kernel(x_ref, ot_ref, gb_ref, ma_ref, wob_ref, wo_ref, fw_ref, out_ref):
    yb = _dot_tn(ot_ref[0], wob_ref[...])
    z = gb_ref[0].astype(_F32) * yb + ma_ref[0].astype(_F32)
    xo = x_ref[0] + _dot(z.astype(_BF16), wo_ref[...])
    ms = jnp.mean(xo * xo, axis=-1, keepdims=True)
    out_ref[0] = xo * lax.rsqrt(ms + NORM_EPS) * fw_ref[...]


def _merge_call(x, ot, gb, ma, wob, wo, fw):
    b, s, d = x.shape
    tm = ROW_TILE
    const = lambda shape: pl.BlockSpec(shape, lambda bi, si: (0,) * len(shape),
                                       pipeline_mode=pl.Buffered(1))
    row_blk = pl.BlockSpec((1, tm, d), lambda bi, si: (bi, si, 0))
    col_blk = pl.BlockSpec((1, d, tm), lambda bi, si: (bi, 0, si))
    return pl.pallas_call(
        _merge_kernel,
        grid=(b, s // tm),
        in_specs=[row_blk, col_blk, row_blk, row_blk,
                  const(wob.shape), const(wo.shape), const(fw.shape)],
        out_specs=row_blk,
        out_shape=jax.ShapeDtypeStruct((b, s, d), x.dtype),
        compiler_params=pltpu.CompilerParams(
            dimension_semantics=("arbitrary", "arbitrary"),
            vmem_limit_bytes=VMEM_LIMIT_BYTES),
        name="merge_out",
    )(x, ot, gb, ma, wob, wo, fw)


def _lambda_init(layer_idx):
    return 0.8 - 0.6 * math.exp(-0.3 * layer_idx)


def kernel(x, norm_w, w_in, conv_w, w_out_a, lambda_q1, lambda_k1, lambda_q2, lambda_k2,
           head_norm_w, w_out_b, b_gate, w_o, final_norm_w):
    depth = norm_w.shape[0]
    assert depth == 1, "the merge kernel fuses the final RMSNorm, so only one layer is supported"
    d = x.shape[-1]
    width = conv_w.shape[-1]
    assert width == d and x.shape[1] % ROW_TILE == 0 and x.shape[1] % Q_TILE == 0
    c1 = 4 * width
    for layer in range(depth):
        lam_init = _lambda_init(layer)
        w = w_in[layer].astype(_BF16)
        wc = w[:, :c1]
        wq, wk, wv, wzb = (w[:, c1 + i * d:c1 + (i + 1) * d] for i in range(4))
        wt = jnp.concatenate([wq.T, wv.T, wzb.T], axis=0)
        wg = w[:, c1 + 4 * d:]
        qt, k, vt, zbt, gb, ma = _proj_call(
            x, norm_w[layer][None, :], wc, wk, wt, wg, b_gate[layer][None, :],
            conv_w[layer], w_out_a[layer].astype(_BF16))
        lam_params = jnp.stack([lambda_q1[layer], lambda_k1[layer],
                                lambda_q2[layer], lambda_k2[layer]], axis=0)
        hnw_b = jnp.broadcast_to(head_norm_w[layer][:, None], (DA_VALUE_DIM, Q_TILE))
        ot = _attn_call(lam_params, hnw_b, qt, k, vt, zbt, lam_init)
        x = _merge_call(x, ot, gb, ma, w_out_b[layer].astype(_BF16),
                        w_o[layer].astype(_BF16), final_norm_w[None, :])
    return x
```

```python
import functools
import math

import jax
import jax.numpy as jnp
from jax import lax
from jax.experimental import pallas as pl
from jax.experimental.pallas import tpu as pltpu

NORM_EPS = 1e-6
HEAD_NORM_EPS = 1e-5
CONV_K = 3
DA_HEADS = 8
DA_HEAD_DIM = 64
DA_VALUE_DIM = 2 * DA_HEAD_DIM

ROW_TILE = 512
CONV_CHUNK = 256
KV_TILE = 256
Q_TILE = 2 * KV_TILE
COL_CHUNK = KV_TILE
SUBLANES = 8
BF16_SUBLANES = 16
VMEM_LIMIT_BYTES = 56 * 1024 * 1024

_BF16 = jnp.bfloat16
_F32 = jnp.float32


def _sigmoid(z):
    return 1.0 / (1.0 + jnp.exp(-z))


def _dot(a, b):
    return jnp.dot(a, b, preferred_element_type=_F32)


def _dot_nt(a, b):
    return lax.dot_general(a, b, (((1,), (1,)), ((), ())), preferred_element_type=_F32)


def _dot_tn(a, b):
    return lax.dot_general(a, b, (((0,), (0,)), ((), ())), preferred_element_type=_F32)


def _proj_kernel(x_ref, nw_ref, wc_ref, wk_ref, wt_ref, wg_ref, bg_ref, cw_ref, woa_ref,
                 qt_ref, k_ref, vt_ref, zbt_ref, gb_ref, ma_ref, carry_ref):
    tm = x_ref.shape[1]
    d = x_ref.shape[2]
    width = woa_ref.shape[0]

    @pl.when(pl.program_id(1) == 0)
    def _():
        carry_ref[...] = jnp.zeros_like(carry_ref)

    xf = x_ref[0]
    ms = jnp.mean(xf * xf, axis=-1, keepdims=True)
    h = (xf * lax.rsqrt(ms + NORM_EPS) * nw_ref[...]).astype(_BF16)

    row8 = lax.broadcasted_iota(jnp.int32, (SUBLANES, CONV_CHUNK), 0)
    ya = jnp.zeros((tm, d), _F32)
    for j in range(width // CONV_CHUNK):
        c0 = j * CONV_CHUNK
        xa = _dot(h, wc_ref[:, 0 * width + c0:0 * width + c0 + CONV_CHUNK])
        cg = _dot(h, wc_ref[:, 1 * width + c0:1 * width + c0 + CONV_CHUNK])
        bgate = _dot(h, wc_ref[:, 2 * width + c0:2 * width + c0 + CONV_CHUNK])
        za = _dot(h, wc_ref[:, 3 * width + c0:3 * width + c0 + CONV_CHUNK])
        u = cg * xa
        prev = carry_ref[:, c0:c0 + CONV_CHUNK]
        carry_ref[:, c0:c0 + CONV_CHUNK] = u[tm - SUBLANES:, :]
        u1 = pltpu.roll(u, 1, 0)
        u2 = pltpu.roll(u, 2, 0)
        p1 = pltpu.roll(prev, 1, 0)
        p2 = pltpu.roll(prev, 2, 0)
        u1 = jnp.concatenate([jnp.where(row8 < 1, p1, u1[:SUBLANES]), u1[SUBLANES:]], axis=0)
        u2 = jnp.concatenate([jnp.where(row8 < 2, p2, u2[:SUBLANES]), u2[SUBLANES:]], axis=0)
        cw = cw_ref[:, c0:c0 + CONV_CHUNK]
        conv = u2 * cw[0:1] + u1 * cw[1:2] + u * cw[2:3]
        gated = (za * _sigmoid(za)) * bgate * conv
        ya = ya + _dot(gated.astype(_BF16), woa_ref[c0:c0 + CONV_CHUNK, :])

    k_ref[0] = _dot(h, wk_ref[...]).astype(_BF16)
    scale = DA_HEAD_DIM ** -0.5 * math.log2(math.e)
    qt_ref[0] = (_dot_nt(wt_ref[0 * d:1 * d, :], h) * scale).astype(_BF16)
    vt_ref[0] = _dot_nt(wt_ref[1 * d:2 * d, :], h).astype(_BF16)
    zbt_ref[0] = _dot_nt(wt_ref[2 * d:3 * d, :], h).astype(_BF16)

    g = _sigmoid(_dot(h, wg_ref[...]) + bg_ref[...])
    ma_ref[0] = (g[:, :d] * ya).astype(_BF16)
    gb_ref[0] = g[:, d:].astype(_BF16)


def _proj_call(x, nw, wc, wk, wt, wg, bg, cw, woa):
    b, s, d = x.shape
    tm = ROW_TILE
    const = lambda shape: pl.BlockSpec(shape, lambda bi, si: (0,) * len(shape),
                                       pipeline_mode=pl.Buffered(1))
    row_blk = pl.BlockSpec((1, tm, d), lambda bi, si: (bi, si, 0))
    col_blk = pl.BlockSpec((1, d, tm), lambda bi, si: (bi, 0, si))
    nat = jax.ShapeDtypeStruct((b, s, d), _BF16)
    tr = jax.ShapeDtypeStruct((b, d, s), _BF16)
    return pl.pallas_call(
        _proj_kernel,
        grid=(b, s // tm),
        in_specs=[row_blk, const(nw.shape), const(wc.shape), const(wk.shape), const(wt.shape),
                  const(wg.shape), const(bg.shape), const(cw.shape), const(woa.shape)],
        out_specs=[col_blk, row_blk, col_blk, col_blk, row_blk, row_blk],
        out_shape=[tr, nat, tr, tr, nat, nat],
        scratch_shapes=[pltpu.VMEM((SUBLANES, d), _F32)],
        compiler_params=pltpu.CompilerParams(
            dimension_semantics=("arbitrary", "arbitrary"),
            vmem_limit_bytes=VMEM_LIMIT_BYTES),
        name="proj_conv_gates",
    )(x, nw, wc, wk, wt, wg, bg, cw, woa)


def _attn_kernel(lam_ref, hnw_ref, qt_ref, k_ref, vt_ref, zbt_ref, o_ref,
                 q2_ref, s_ref, tri_ref, acc_ref, m_ref, *, lam_init):
    s_len = k_ref.shape[1]
    tq, tk, cw = Q_TILE, KV_TILE, COL_CHUNK
    hd = DA_VALUE_DIM
    nq = s_len // tq
    n_chunks = 2 * tq // cw
    chunks_per_half = tq // cw

    lp = lam_ref[...]
    lam = (jnp.exp(jnp.sum(lp[0:1] * lp[1:2], axis=1, keepdims=True))
           - jnp.exp(jnp.sum(lp[2:3] * lp[3:4], axis=1, keepdims=True)) + lam_init)

    kv_idx = lax.broadcasted_iota(jnp.int32, (tk, cw), 0)
    q_idx = lax.broadcasted_iota(jnp.int32, (tk, cw), 1)
    tri_ref[...] = jnp.where(kv_idx <= q_idx, 0.0, -jnp.inf).astype(_BF16)
    ones_rows = jnp.ones((BF16_SUBLANES, tk), _BF16)
    row_q = lax.broadcasted_iota(jnp.int32, (hd, tq), 0)

    def build_q2(i):
        qt = qt_ref[0, :, i * tq:(i + 1) * tq]
        zero = jnp.zeros_like(qt)
        q2_ref[i % 2] = jnp.concatenate([jnp.where(row_q < DA_HEAD_DIM, qt, zero),
                                         jnp.where(row_q >= DA_HEAD_DIM, qt, zero)], axis=1)

    def live_chunks(i, j):
        kv0 = j * tk
        return [c for c in range(n_chunks) if kv0 < i * tq + (c % chunks_per_half + 1) * cw]

    def scores(i, j):
        kb = k_ref[0, j * tk:(j + 1) * tk, :]
        for c in live_chunks(i, j):
            cols = slice(c * cw, (c + 1) * cw)
            s_ref[j % 2, :, cols] = _dot(kb, q2_ref[i % 2, :, cols])

    def softmax_pv(i, j):
        vb = jnp.concatenate([vt_ref[0, :, j * tk:(j + 1) * tk], ones_rows], axis=0)
        for c in live_chunks(i, j):
            cols = slice(c * cw, (c + 1) * cw)
            s = s_ref[j % 2, :, cols].astype(_BF16)
            if j * tk == i * tq + (c % chunks_per_half) * cw:
                s = s + tri_ref[...]
            m_old = m_ref[:, cols]
            m_new = jnp.maximum(m_old, jnp.max(s, axis=0, keepdims=True).astype(_F32))
            m_ref[:, cols] = m_new
            p = jnp.exp2((s - m_new.astype(_BF16)).astype(_F32)).astype(_BF16)
            alpha = jnp.exp2(m_old - m_new)
            acc_ref[:, cols] = acc_ref[:, cols] * alpha + _dot(vb, p)

    def finalize(i):
        inv = 1.0 / acc_ref[hd:hd + 1, :]
        acc = acc_ref[:hd, :]
        o = acc[:, :tq] * inv[:, :tq] - lam * (acc[:, tq:] * inv[:, tq:])
        ms = jnp.mean(o * o, axis=0, keepdims=True)
        on = (o * lax.rsqrt(ms + HEAD_NORM_EPS) * hnw_ref[...]) * (1.0 - lam_init)
        zb = zbt_ref[0, :, i * tq:(i + 1) * tq].astype(_F32)
        o_ref[0, :, i * tq:(i + 1) * tq] = ((zb * _sigmoid(zb)) * on).astype(_BF16)

    build_q2(0)
    scores(0, 0)
    for i in range(nq):
        n_blocks = (i + 1) * tq // tk
        m_ref[...] = jnp.full_like(m_ref, -jnp.inf)
        acc_ref[...] = jnp.zeros_like(acc_ref)
        for j in range(n_blocks):
            if j + 1 < n_blocks:
                scores(i, j + 1)
            elif i + 1 < nq:
                build_q2(i + 1)
                scores(i + 1, 0)
            softmax_pv(i, j)
        finalize(i)


def _attn_call(lam_params, hnw_b, qt, k, vt, zbt, lam_init):
    b, d, s = qt.shape
    hd = DA_VALUE_DIM
    assert COL_CHUNK == KV_TILE and Q_TILE % COL_CHUNK == 0 and s % Q_TILE == 0
    col_blk = pl.BlockSpec((1, hd, s), lambda bi, hi: (bi, hi, 0))
    return pl.pallas_call(
        functools.partial(_attn_kernel, lam_init=lam_init),
        grid=(b, DA_HEADS),
        in_specs=[pl.BlockSpec(lam_params.shape, lambda bi, hi: (0, 0)),
                  pl.BlockSpec(hnw_b.shape, lambda bi, hi: (0, 0)),
                  col_blk,
                  pl.BlockSpec((1, s, hd), lambda bi, hi: (bi, 0, hi)),
                  col_blk, col_blk],
        out_specs=col_blk,
        out_shape=jax.ShapeDtypeStruct((b, d, s), _BF16),
        scratch_shapes=[pltpu.VMEM((2, hd, 2 * Q_TILE), _BF16),
                        pltpu.VMEM((2, KV_TILE, 2 * Q_TILE), _F32),
                        pltpu.VMEM((KV_TILE, COL_CHUNK), _BF16),
                        pltpu.VMEM((hd + BF16_SUBLANES, 2 * Q_TILE), _F32),
                        pltpu.VMEM((1, 2 * Q_TILE), _F32)],
        compiler_params=pltpu.CompilerParams(
            dimension_semantics=("arbitrary", "arbitrary"),
            vmem_limit_bytes=VMEM_LIMIT_BYTES),
        name="diff_attention",
    )(lam_params, hnw_b, qt, k, vt, zbt)


def _merge_kernel(x_ref, ot_ref, gb_ref, ma_ref, wob_ref, wo_ref, fw_ref, out_ref):
    yb = _dot_tn(ot_ref[0], wob_ref[...])
    z = gb_ref[0].astype(_F32) * yb + ma_ref[0].astype(_F32)
    xo = x_ref[0] + _dot(z.astype(_BF16), wo_ref[...])
    ms = jnp.mean(xo * xo, axis=-1, keepdims=True)
    out_ref[0] = xo * lax.rsqrt(ms + NORM_EPS) * fw_ref[...]


def _merge_call(x, ot, gb, ma, wob, wo, fw):
    b, s, d = x.shape
    tm = ROW_TILE
    const = lambda shape: pl.BlockSpec(shape, lambda bi, si: (0,) * len(shape),
                                       pipeline_mode=pl.Buffered(1))
    row_blk = pl.BlockSpec((1, tm, d), lambda bi, si: (bi, si, 0))
    col_blk = pl.BlockSpec((1, d, tm), lambda bi, si: (bi, 0, si))
    return pl.pallas_call(
        _merge_kernel,
        grid=(b, s // tm),
        in_specs=[row_blk, col_blk, row_blk, row_blk,
                  const(wob.shape), const(wo.shape), const(fw.shape)],
        out_specs=row_blk,
        out_shape=jax.ShapeDtypeStruct((b, s, d), x.dtype),
        compiler_params=pltpu.CompilerParams(
            dimension_semantics=("arbitrary", "arbitrary"),
            vmem_limit_bytes=VMEM_LIMIT_BYTES),
        name="merge_out",
    )(x, ot, gb, ma, wob, wo, fw)


def _lambda_init(layer_idx):
    return 0.8 - 0.6 * math.exp(-0.3 * layer_idx)


def kernel(x, norm_w, w_in, conv_w, w_out_a, lambda_q1, lambda_k1, lambda_q2, lambda_k2,
           head_norm_w, w_out_b, b_gate, w_o, final_norm_w):
    depth = norm_w.shape[0]
    assert depth == 1, "the merge kernel fuses the final RMSNorm, so only one layer is supported"
    d = x.shape[-1]
    width = conv_w.shape[-1]
    assert width == d and x.shape[1] % ROW_TILE == 0 and x.shape[1] % Q_TILE == 0
    c1 = 4 * width
    for layer in range(depth):
        lam_init = _lambda_init(layer)
        w = w_in[layer].astype(_BF16)
        wc = w[:, :c1]
        wq, wk, wv, wzb = (w[:, c1 + i * d:c1 + (i + 1) * d] for i in range(4))
        wt = jnp.concatenate([wq.T, wv.T, wzb.T], axis=0)
        wg = w[:, c1 + 4 * d:]
        qt, k, vt, zbt, gb, ma = _proj_call(
            x, norm_w[layer][None, :], wc, wk, wt, wg, b_gate[layer][None, :],
            conv_w[layer], w_out_a[layer].astype(_BF16))
        lam_params = jnp.stack([lambda_q1[layer], lambda_k1[layer],
                                lambda_q2[layer], lambda_k2[layer]], axis=0)
        hnw_b = jnp.broadcast_to(head_norm_w[layer][:, None], (DA_VALUE_DIM, Q_TILE))
        ot = _attn_call(lam_params, hnw_b, qt, k, vt, zbt, lam_init)
        x = _merge_call(x, ot, gb, ma, w_out_b[layer].astype(_BF16),
                        w_o[layer].astype(_BF16), final_norm_w[None, :])
    return x
```

```python
import functools
import math

import jax
import jax.numpy as jnp
from jax import lax
from jax.experimental import pallas as pl
from jax.experimental.pallas import tpu as pltpu

NORM_EPS = 1e-6
HEAD_NORM_EPS = 1e-5
CONV_K = 3
DA_HEADS = 8
DA_HEAD_DIM = 64
DA_VALUE_DIM = 2 * DA_HEAD_DIM

ROW_TILE = 512
MERGE_ROW_TILE = 1024
CONV_CHUNK = 256
KV_TILE = 256
Q_TILE = 2 * KV_TILE
COL_CHUNK = KV_TILE
SUBLANES = 8
BF16_SUBLANES = 16
VMEM_LIMIT_BYTES = 56 * 1024 * 1024

_BF16 = jnp.bfloat16
_F32 = jnp.float32


def _sigmoid(z):
    return 1.0 / (1.0 + jnp.exp(-z))


def _dot(a, b):
    return jnp.dot(a, b, preferred_element_type=_F32)


def _dot_nt(a, b):
    return lax.dot_general(a, b, (((1,), (1,)), ((), ())), preferred_element_type=_F32)


def _dot_tn(a, b):
    return lax.dot_general(a, b, (((0,), (0,)), ((), ())), preferred_element_type=_F32)


def _proj_kernel(x_ref, nw_ref, wc_ref, wk_ref, wt_ref, wg_ref, bg_ref, cw_ref, woa_ref,
                 qt_ref, k_ref, vt_ref, zbt_ref, gb_ref, ma_ref, carry_ref):
    tm = x_ref.shape[1]
    d = x_ref.shape[2]
    width = woa_ref.shape[0]

    @pl.when(pl.program_id(1) == 0)
    def _():
        carry_ref[...] = jnp.zeros_like(carry_ref)

    xf = x_ref[0]
    ms = jnp.mean(xf * xf, axis=-1, keepdims=True)
    h = (xf * lax.rsqrt(ms + NORM_EPS) * nw_ref[...]).astype(_BF16)

    g = _sigmoid(_dot(h, wg_ref[...]) + bg_ref[...])
    gb_ref[0] = g[:, d:].astype(_BF16)
    ga = g[:, :d]
    zbt = _dot_nt(wt_ref[2 * d:3 * d, :], h)
    zbt_ref[0] = (zbt * _sigmoid(zbt)).astype(_BF16)

    row8 = lax.broadcasted_iota(jnp.int32, (SUBLANES, CONV_CHUNK), 0)
    ya = jnp.zeros((tm, d), _F32)
    for j in range(width // CONV_CHUNK):
        c0 = j * CONV_CHUNK
        xa = _dot(h, wc_ref[:, 0 * width + c0:0 * width + c0 + CONV_CHUNK])
        cg = _dot(h, wc_ref[:, 1 * width + c0:1 * width + c0 + CONV_CHUNK])
        bgate = _dot(h, wc_ref[:, 2 * width + c0:2 * width + c0 + CONV_CHUNK])
        za = _dot(h, wc_ref[:, 3 * width + c0:3 * width + c0 + CONV_CHUNK])
        u = cg * xa
        prev = carry_ref[:, c0:c0 + CONV_CHUNK]
        carry_ref[:, c0:c0 + CONV_CHUNK] = u[tm - SUBLANES:, :]
        u1 = pltpu.roll(u, 1, 0)
        u2 = pltpu.roll(u, 2, 0)
        p1 = pltpu.roll(prev, 1, 0)
        p2 = pltpu.roll(prev, 2, 0)
        u1 = jnp.concatenate([jnp.where(row8 < 1, p1, u1[:SUBLANES]), u1[SUBLANES:]], axis=0)
        u2 = jnp.concatenate([jnp.where(row8 < 2, p2, u2[:SUBLANES]), u2[SUBLANES:]], axis=0)
        cw = cw_ref[:, c0:c0 + CONV_CHUNK]
        conv = u2 * cw[0:1] + u1 * cw[1:2] + u * cw[2:3]
        gated = (za * _sigmoid(za)) * bgate * conv
        ya = ya + _dot(gated.astype(_BF16), woa_ref[c0:c0 + CONV_CHUNK, :])
    ma_ref[0] = (ga * ya).astype(_BF16)

    k_ref[0] = _dot(h, wk_ref[...]).astype(_BF16)
    scale = DA_HEAD_DIM ** -0.5 * math.log2(math.e)
    qt_ref[0] = (_dot_nt(wt_ref[0 * d:1 * d, :], h) * scale).astype(_BF16)
    vt_ref[0] = _dot_nt(wt_ref[1 * d:2 * d, :], h).astype(_BF16)


def _proj_call(x, nw, wc, wk, wt, wg, bg, cw, woa):
    b, s, d = x.shape
    tm = ROW_TILE
    const = lambda shape: pl.BlockSpec(shape, lambda bi, si: (0,) * len(shape),
                                       pipeline_mode=pl.Buffered(1))
    row_blk = pl.BlockSpec((1, tm, d), lambda bi, si: (bi, si, 0))
    col_blk = pl.BlockSpec((1, d, tm), lambda bi, si: (bi, 0, si))
    nat = jax.ShapeDtypeStruct((b, s, d), _BF16)
    tr = jax.ShapeDtypeStruct((b, d, s), _BF16)
    return pl.pallas_call(
        _proj_kernel,
        grid=(b, s // tm),
        in_specs=[row_blk, const(nw.shape), const(wc.shape), const(wk.shape), const(wt.shape),
                  const(wg.shape), const(bg.shape), const(cw.shape), const(woa.shape)],
        out_specs=[col_blk, row_blk, col_blk, col_blk, row_blk, row_blk],
        out_shape=[tr, nat, tr, tr, nat, nat],
        scratch_shapes=[pltpu.VMEM((SUBLANES, d), _F32)],
        compiler_params=pltpu.CompilerParams(
            dimension_semantics=("arbitrary", "arbitrary"),
            vmem_limit_bytes=VMEM_LIMIT_BYTES),
        name="proj_conv_gates",
    )(x, nw, wc, wk, wt, wg, bg, cw, woa)


def _attn_kernel(lam_ref, hnw_ref, qt_ref, k_ref, vt_ref, gate_ref, o_ref,
                 q2_ref, s_ref, tri_ref, acc_ref, m_ref, *, lam_init):
    s_len = k_ref.shape[1]
    tq, tk, cw = Q_TILE, KV_TILE, COL_CHUNK
    hd = DA_VALUE_DIM
    nq = s_len // tq
    n_chunks = 2 * tq // cw
    chunks_per_half = tq // cw

    lp = lam_ref[...]
    lam = (jnp.exp(jnp.sum(lp[0:1] * lp[1:2], axis=1, keepdims=True))
           - jnp.exp(jnp.sum(lp[2:3] * lp[3:4], axis=1, keepdims=True)) + lam_init)

    kv_idx = lax.broadcasted_iota(jnp.int32, (tk, cw), 0)
    q_idx = lax.broadcasted_iota(jnp.int32, (tk, cw), 1)
    tri_ref[...] = jnp.where(kv_idx <= q_idx, 0.0, -jnp.inf).astype(_BF16)
    ones_rows = jnp.ones((BF16_SUBLANES, tk), _BF16)
    row_q = lax.broadcasted_iota(jnp.int32, (hd, tq), 0)

    def build_q2(i):
        qt = qt_ref[0, :, i * tq:(i + 1) * tq]
        zero = jnp.zeros_like(qt)
        q2_ref[i % 2] = jnp.concatenate([jnp.where(row_q < DA_HEAD_DIM, qt, zero),
                                         jnp.where(row_q >= DA_HEAD_DIM, qt, zero)], axis=1)

    def live_chunks(i, j):
        kv0 = j * tk
        return [c for c in range(n_chunks) if kv0 < i * tq + (c % chunks_per_half + 1) * cw]

    def scores(i, j):
        kb = k_ref[0, j * tk:(j + 1) * tk, :]
        for c in live_chunks(i, j):
            cols = slice(c * cw, (c + 1) * cw)
            s_ref[j % 2, :, cols] = _dot(kb, q2_ref[i % 2, :, cols])

    def softmax_pv(i, j):
        vb = jnp.concatenate([vt_ref[0, :, j * tk:(j + 1) * tk], ones_rows], axis=0)
        for c in live_chunks(i, j):
            cols = slice(c * cw, (c + 1) * cw)
            s = s_ref[j % 2, :, cols].astype(_BF16)
            if j * tk == i * tq + (c % chunks_per_half) * cw:
                s = s + tri_ref[...]
            m_old = m_ref[:, cols]
            m_new = jnp.maximum(m_old, jnp.max(s, axis=0, keepdims=True).astype(_F32))
            m_ref[:, cols] = m_new
            p = jnp.exp2((s - m_new.astype(_BF16)).astype(_F32)).astype(_BF16)
            alpha = jnp.exp2(m_old - m_new)
            acc_ref[:, cols] = acc_ref[:, cols] * alpha + _dot(vb, p)

    def finalize(i):
        inv = 1.0 / acc_ref[hd:hd + 1, :]
        acc = acc_ref[:hd, :]
        o = acc[:, :tq] * inv[:, :tq] - acc[:, tq:] * (lam * inv[:, tq:])
        ms = jnp.mean(o * o, axis=0, keepdims=True)
        on = o * (lax.rsqrt(ms + HEAD_NORM_EPS) * (1.0 - lam_init)) * hnw_ref[...]
        gate = gate_ref[0, :, i * tq:(i + 1) * tq].astype(_F32)
        o_ref[0, :, i * tq:(i + 1) * tq] = (gate * on).astype(_BF16)

    build_q2(0)
    scores(0, 0)
    for i in range(nq):
        n_blocks = (i + 1) * tq // tk
        m_ref[...] = jnp.full_like(m_ref, -jnp.inf)
        acc_ref[...] = jnp.zeros_like(acc_ref)
        for j in range(n_blocks):
            if j + 1 < n_blocks:
                scores(i, j + 1)
            elif i + 1 < nq:
                build_q2(i + 1)
                scores(i + 1, 0)
            softmax_pv(i, j)
        finalize(i)


def _attn_call(lam_params, hnw_b, qt, k, vt, zbt, lam_init):
    b, d, s = qt.shape
    hd = DA_VALUE_DIM
    assert COL_CHUNK == KV_TILE and Q_TILE % COL_CHUNK == 0 and s % Q_TILE == 0
    col_blk = pl.BlockSpec((1, hd, s), lambda bi, hi: (bi, hi, 0))
    return pl.pallas_call(
        functools.partial(_attn_kernel, lam_init=lam_init),
        grid=(b, DA_HEADS),
        in_specs=[pl.BlockSpec(lam_params.shape, lambda bi, hi: (0, 0)),
                  pl.BlockSpec(hnw_b.shape, lambda bi, hi: (0, 0)),
                  col_blk,
                  pl.BlockSpec((1, s, hd), lambda bi, hi: (bi, 0, hi)),
                  col_blk, col_blk],
        out_specs=col_blk,
        out_shape=jax.ShapeDtypeStruct((b, d, s), _BF16),
        scratch_shapes=[pltpu.VMEM((2, hd, 2 * Q_TILE), _BF16),
                        pltpu.VMEM((2, KV_TILE, 2 * Q_TILE), _F32),
                        pltpu.VMEM((KV_TILE, COL_CHUNK), _BF16),
                        pltpu.VMEM((hd + BF16_SUBLANES, 2 * Q_TILE), _F32),
                        pltpu.VMEM((1, 2 * Q_TILE), _F32)],
        compiler_params=pltpu.CompilerParams(
            dimension_semantics=("arbitrary", "arbitrary"),
            vmem_limit_bytes=VMEM_LIMIT_BYTES),
        name="diff_attention",
    )(lam_params, hnw_b, qt, k, vt, zbt)


def _merge_kernel(x_ref, ot_ref, gb_ref, ma_ref, wob_ref, wo_ref, fw_ref, out_ref):
    yb = _dot_tn(ot_ref[0], wob_ref[...])
    z = gb_ref[0].astype(_F32) * yb + ma_ref[0].astype(_F32)
    xo = x_ref[0] + _dot(z.astype(_BF16), wo_ref[...])
    ms = jnp.mean(xo * xo, axis=-1, keepdims=True)
    out_ref[0] = xo * lax.rsqrt(ms + NORM_EPS) * fw_ref[...]


def _merge_call(x, ot, gb, ma, wob, wo, fw):
    b, s, d = x.shape
    tm = MERGE_ROW_TILE
    const = lambda shape: pl.BlockSpec(shape, lambda bi, si: (0,) * len(shape),
                                       pipeline_mode=pl.Buffered(1))
    row_blk = pl.BlockSpec((1, tm, d), lambda bi, si: (bi, si, 0))
    col_blk = pl.BlockSpec((1, d, tm), lambda bi, si: (bi, 0, si))
    return pl.pallas_call(
        _merge_kernel,
        grid=(b, s // tm),
        in_specs=[row_blk, col_blk, row_blk, row_blk,
                  const(wob.shape), const(wo.shape), const(fw.shape)],
        out_specs=row_blk,
        out_shape=jax.ShapeDtypeStruct((b, s, d), x.dtype),
        compiler_params=pltpu.CompilerParams(
            dimension_semantics=("arbitrary", "arbitrary"),
            vmem_limit_bytes=VMEM_LIMIT_BYTES),
        name="merge_out",
    )(x, ot, gb, ma, wob, wo, fw)


def _lambda_init(layer_idx):
    return 0.8 - 0.6 * math.exp(-0.3 * layer_idx)


def kernel(x, norm_w, w_in, conv_w, w_out_a, lambda_q1, lambda_k1, lambda_q2, lambda_k2,
           head_norm_w, w_out_b, b_gate, w_o, final_norm_w):
    depth = norm_w.shape[0]
    assert depth == 1, "the merge kernel fuses the final RMSNorm, so only one layer is supported"
    d = x.shape[-1]
    width = conv_w.shape[-1]
    assert width == d and x.shape[1] % ROW_TILE == 0 and x.shape[1] % Q_TILE == 0
    c1 = 4 * width
    for layer in range(depth):
        lam_init = _lambda_init(layer)
        w = w_in[layer].astype(_BF16)
        wc = w[:, :c1]
        wq, wk, wv, wzb = (w[:, c1 + i * d:c1 + (i + 1) * d] for i in range(4))
        wt = jnp.concatenate([wq.T, wv.T, wzb.T], axis=0)
        wg = w[:, c1 + 4 * d:]
        qt, k, vt, zbt, gb, ma = _proj_call(
            x, norm_w[layer][None, :], wc, wk, wt, wg, b_gate[layer][None, :],
            conv_w[layer], w_out_a[layer].astype(_BF16))
        lam_params = jnp.stack([lambda_q1[layer], lambda_k1[layer],
                                lambda_q2[layer], lambda_k2[layer]], axis=0)
        hnw_b = jnp.broadcast_to(head_norm_w[layer][:, None], (DA_VALUE_DIM, Q_TILE))
        ot = _attn_call(lam_params, hnw_b, qt, k, vt, zbt, lam_init)
        x = _merge_call(x, ot, gb, ma, w_out_b[layer].astype(_BF16),
                        w_o[layer].astype(_BF16), final_norm_w[None, :])
    return x
```

```python
import functools
import math

import jax
import jax.numpy as jnp
from jax import lax
from jax.experimental import pallas as pl
from jax.experimental.pallas import tpu as pltpu

NORM_EPS = 1e-6
HEAD_NORM_EPS = 1e-5
CONV_K = 3
DA_HEADS = 8
DA_HEAD_DIM = 64
DA_VALUE_DIM = 2 * DA_HEAD_DIM

ROW_TILE = 512
MERGE_ROW_TILE = 1024
CONV_CHUNK = 256
KV_TILE = 256
Q_TILE = 2 * KV_TILE
COL_CHUNK = KV_TILE
SCORE_SLOTS = 10
SUBLANES = 8
BF16_SUBLANES = 16
VMEM_LIMIT_BYTES = 56 * 1024 * 1024

_BF16 = jnp.bfloat16
_F32 = jnp.float32


def _sigmoid(z):
    return 1.0 / (1.0 + jnp.exp(-z))


def _dot(a, b):
    return jnp.dot(a, b, preferred_element_type=_F32)


def _dot_nt(a, b):
    return lax.dot_general(a, b, (((1,), (1,)), ((), ())), preferred_element_type=_F32)


def _dot_tn(a, b):
    return lax.dot_general(a, b, (((0,), (0,)), ((), ())), preferred_element_type=_F32)


def _proj_kernel(x_ref, nw_ref, wc_ref, wk_ref, wt_ref, wg_ref, bg_ref, cw_ref, woa_ref,
                 qt_ref, k_ref, vt_ref, zbt_ref, gb_ref, ma_ref, carry_ref):
    tm = x_ref.shape[1]
    d = x_ref.shape[2]
    width = woa_ref.shape[0]

    @pl.when(pl.program_id(1) == 0)
    def _():
        carry_ref[...] = jnp.zeros_like(carry_ref)

    xf = x_ref[0]
    ms = jnp.mean(xf * xf, axis=-1, keepdims=True)
    h = (xf * lax.rsqrt(ms + NORM_EPS) * nw_ref[...]).astype(_BF16)

    g = _sigmoid(_dot(h, wg_ref[...]) + bg_ref[...])
    gb_ref[0] = g[:, d:].astype(_BF16)
    ga = g[:, :d]
    zbt = _dot_nt(wt_ref[2 * d:3 * d, :], h)
    zbt_ref[0] = (zbt * _sigmoid(zbt)).astype(_BF16)

    row8 = lax.broadcasted_iota(jnp.int32, (SUBLANES, CONV_CHUNK), 0)
    ya = jnp.zeros((tm, d), _F32)
    for j in range(width // CONV_CHUNK):
        c0 = j * CONV_CHUNK
        xa = _dot(h, wc_ref[:, 0 * width + c0:0 * width + c0 + CONV_CHUNK])
        cg = _dot(h, wc_ref[:, 1 * width + c0:1 * width + c0 + CONV_CHUNK])
        bgate = _dot(h, wc_ref[:, 2 * width + c0:2 * width + c0 + CONV_CHUNK])
        za = _dot(h, wc_ref[:, 3 * width + c0:3 * width + c0 + CONV_CHUNK])
        u = cg * xa
        prev = carry_ref[:, c0:c0 + CONV_CHUNK]
        carry_ref[:, c0:c0 + CONV_CHUNK] = u[tm - SUBLANES:, :]
        u1 = pltpu.roll(u, 1, 0)
        u2 = pltpu.roll(u, 2, 0)
        p1 = pltpu.roll(prev, 1, 0)
        p2 = pltpu.roll(prev, 2, 0)
        u1 = jnp.concatenate([jnp.where(row8 < 1, p1, u1[:SUBLANES]), u1[SUBLANES:]], axis=0)
        u2 = jnp.concatenate([jnp.where(row8 < 2, p2, u2[:SUBLANES]), u2[SUBLANES:]], axis=0)
        cw = cw_ref[:, c0:c0 + CONV_CHUNK]
        conv = u2 * cw[0:1] + u1 * cw[1:2] + u * cw[2:3]
        gated = (za * _sigmoid(za)) * bgate * conv
        ya = ya + _dot(gated.astype(_BF16), woa_ref[c0:c0 + CONV_CHUNK, :])
    ma_ref[0] = (ga * ya).astype(_BF16)

    k_ref[0] = _dot(h, wk_ref[...]).astype(_BF16)
    scale = DA_HEAD_DIM ** -0.5 * math.log2(math.e)
    qt_ref[0] = (_dot_nt(wt_ref[0 * d:1 * d, :], h) * scale).astype(_BF16)
    vt_ref[0] = _dot_nt(wt_ref[1 * d:2 * d, :], h).astype(_BF16)


def _proj_call(x, nw, wc, wk, wt, wg, bg, cw, woa):
    b, s, d = x.shape
    tm = ROW_TILE
    const = lambda shape: pl.BlockSpec(shape, lambda bi, si: (0,) * len(shape),
                                       pipeline_mode=pl.Buffered(1))
    row_blk = pl.BlockSpec((1, tm, d), lambda bi, si: (bi, si, 0))
    col_blk = pl.BlockSpec((1, d, tm), lambda bi, si: (bi, 0, si))
    nat = jax.ShapeDtypeStruct((b, s, d), _BF16)
    tr = jax.ShapeDtypeStruct((b, d, s), _BF16)
    return pl.pallas_call(
        _proj_kernel,
        grid=(b, s // tm),
        in_specs=[row_blk, const(nw.shape), const(wc.shape), const(wk.shape), const(wt.shape),
                  const(wg.shape), const(bg.shape), const(cw.shape), const(woa.shape)],
        out_specs=[col_blk, row_blk, col_blk, col_blk, row_blk, row_blk],
        out_shape=[tr, nat, tr, tr, nat, nat],
        scratch_shapes=[pltpu.VMEM((SUBLANES, d), _F32)],
        compiler_params=pltpu.CompilerParams(
            dimension_semantics=("arbitrary", "arbitrary"),
            vmem_limit_bytes=VMEM_LIMIT_BYTES),
        name="proj_conv_gates",
    )(x, nw, wc, wk, wt, wg, bg, cw, woa)


def _attn_kernel(lam_ref, hnw_ref, qt_ref, k_ref, vt_ref, gate_ref, o_ref,
                 q2_ref, s_ref, tri_ref, acc_ref, m_ref, *, lam_init):
    s_len = k_ref.shape[1]
    tq, tk, cw = Q_TILE, KV_TILE, COL_CHUNK
    hd = DA_VALUE_DIM
    nq = s_len // tq
    n_chunks = 2 * tq // cw
    chunks_per_half = tq // cw

    lp = lam_ref[...]
    lam = (jnp.exp(jnp.sum(lp[0:1] * lp[1:2], axis=1, keepdims=True))
           - jnp.exp(jnp.sum(lp[2:3] * lp[3:4], axis=1, keepdims=True)) + lam_init)

    kv_idx = lax.broadcasted_iota(jnp.int32, (tk, cw), 0)
    q_idx = lax.broadcasted_iota(jnp.int32, (tk, cw), 1)
    tri_ref[...] = jnp.where(kv_idx <= q_idx, 0.0, -jnp.inf)
    ones_rows = jnp.ones((BF16_SUBLANES, tk), _BF16)
    row_q = lax.broadcasted_iota(jnp.int32, (hd, tq), 0)

    def build_q2(i):
        qt = qt_ref[0, :, i * tq:(i + 1) * tq]
        zero = jnp.zeros_like(qt)
        q2_ref[i % 2] = jnp.concatenate([jnp.where(row_q < DA_HEAD_DIM, qt, zero),
                                         jnp.where(row_q >= DA_HEAD_DIM, qt, zero)], axis=1)

    def chunk_start(i, c):
        return i * tq + (c % chunks_per_half) * cw

    steps = [(i, j, c) for i in range(nq) for j in range((i + 1) * tq // tk)
             for c in range(n_chunks) if j * tk < chunk_start(i, c) + cw]

    def scores(n):
        i, j, c = steps[n]
        if j == 0 and c == 0:
            build_q2(i)
        kb = k_ref[0, j * tk:(j + 1) * tk, :]
        s_ref[n % SCORE_SLOTS] = _dot(kb, q2_ref[i % 2, :, c * cw:(c + 1) * cw])

    def softmax_pv(n):
        i, j, c = steps[n]
        cols = slice(c * cw, (c + 1) * cw)
        vb = jnp.concatenate([vt_ref[0, :, j * tk:(j + 1) * tk], ones_rows], axis=0)
        s = s_ref[n % SCORE_SLOTS]
        if j * tk == chunk_start(i, c):
            s = s + tri_ref[...]
        m_old = m_ref[:, cols]
        m_new = jnp.maximum(m_old, jnp.max(s.astype(_BF16), axis=0, keepdims=True).astype(_F32))
        m_ref[:, cols] = m_new
        p = jnp.exp2(s - m_new).astype(_BF16)
        alpha = jnp.exp2(m_old - m_new)
        acc_ref[:, cols] = acc_ref[:, cols] * alpha + _dot(vb, p)

    def finalize(i):
        inv = 1.0 / acc_ref[hd:hd + 1, :]
        acc = acc_ref[:hd, :]
        o = acc[:, :tq] * inv[:, :tq] - acc[:, tq:] * (lam * inv[:, tq:])
        ms = jnp.mean(o * o, axis=0, keepdims=True)
        on = o * (lax.rsqrt(ms + HEAD_NORM_EPS) * (1.0 - lam_init)) * hnw_ref[...]
        gate = gate_ref[0, :, i * tq:(i + 1) * tq].astype(_F32)
        o_ref[0, :, i * tq:(i + 1) * tq] = (gate * on).astype(_BF16)

    ahead = SCORE_SLOTS - 1
    for n in range(ahead):
        scores(n)
    for n, (i, j, c) in enumerate(steps):
        if j == 0 and c == 0:
            m_ref[...] = jnp.full_like(m_ref, -jnp.inf)
            acc_ref[...] = jnp.zeros_like(acc_ref)
        if n + ahead < len(steps):
            scores(n + ahead)
        softmax_pv(n)
        if n + 1 == len(steps) or steps[n + 1][0] != i:
            finalize(i)


def _attn_call(lam_params, hnw_b, qt, k, vt, zbt, lam_init):
    b, d, s = qt.shape
    hd = DA_VALUE_DIM
    assert COL_CHUNK == KV_TILE and Q_TILE % COL_CHUNK == 0 and s % Q_TILE == 0
    col_blk = pl.BlockSpec((1, hd, s), lambda bi, hi: (bi, hi, 0))
    return pl.pallas_call(
        functools.partial(_attn_kernel, lam_init=lam_init),
        grid=(b, DA_HEADS),
        in_specs=[pl.BlockSpec(lam_params.shape, lambda bi, hi: (0, 0)),
                  pl.BlockSpec(hnw_b.shape, lambda bi, hi: (0, 0)),
                  col_blk,
                  pl.BlockSpec((1, s, hd), lambda bi, hi: (bi, 0, hi)),
                  col_blk, col_blk],
        out_specs=col_blk,
        out_shape=jax.ShapeDtypeStruct((b, d, s), _BF16),
        scratch_shapes=[pltpu.VMEM((2, hd, 2 * Q_TILE), _BF16),
                        pltpu.VMEM((SCORE_SLOTS, KV_TILE, COL_CHUNK), _F32),
                        pltpu.VMEM((KV_TILE, COL_CHUNK), _F32),
                        pltpu.VMEM((hd + BF16_SUBLANES, 2 * Q_TILE), _F32),
                        pltpu.VMEM((1, 2 * Q_TILE), _F32)],
        compiler_params=pltpu.CompilerParams(
            dimension_semantics=("arbitrary", "arbitrary"),
            vmem_limit_bytes=VMEM_LIMIT_BYTES),
        name="diff_attention",
    )(lam_params, hnw_b, qt, k, vt, zbt)


def _merge_kernel(x_ref, ot_ref, gb_ref, ma_ref, wob_ref, wo_ref, fw_ref, out_ref):
    yb = _dot_tn(ot_ref[0], wob_ref[...])
    z = gb_ref[0].astype(_F32) * yb + ma_ref[0].astype(_F32)
    xo = x_ref[0] + _dot(z.astype(_BF16), wo_ref[...])
    ms = jnp.mean(xo * xo, axis=-1, keepdims=True)
    out_ref[0] = xo * lax.rsqrt(ms + NORM_EPS) * fw_ref[...]


def _merge_call(x, ot, gb, ma, wob, wo, fw):
    b, s, d = x.shape
    tm = MERGE_ROW_TILE
    const = lambda shape: pl.BlockSpec(shape, lambda bi, si: (0,) * len(shape),
                                       pipeline_mode=pl.Buffered(1))
    row_blk = pl.BlockSpec((1, tm, d), lambda bi, si: (bi, si, 0))
    col_blk = pl.BlockSpec((1, d, tm), lambda bi, si: (bi, 0, si))
    return pl.pallas_call(
        _merge_kernel,
        grid=(b, s // tm),
        in_specs=[row_blk, col_blk, row_blk, row_blk,
                  const(wob.shape), const(wo.shape), const(fw.shape)],
        out_specs=row_blk,
        out_shape=jax.ShapeDtypeStruct((b, s, d), x.dtype),
        compiler_params=pltpu.CompilerParams(
            dimension_semantics=("arbitrary", "arbitrary"),
            vmem_limit_bytes=VMEM_LIMIT_BYTES),
        name="merge_out",
    )(x, ot, gb, ma, wob, wo, fw)


def _lambda_init(layer_idx):
    return 0.8 - 0.6 * math.exp(-0.3 * layer_idx)


def kernel(x, norm_w, w_in, conv_w, w_out_a, lambda_q1, lambda_k1, lambda_q2, lambda_k2,
           head_norm_w, w_out_b, b_gate, w_o, final_norm_w):
    depth = norm_w.shape[0]
    assert depth == 1, "the merge kernel fuses the final RMSNorm, so only one layer is supported"
    d = x.shape[-1]
    width = conv_w.shape[-1]
    assert width == d and x.shape[1] % ROW_TILE == 0 and x.shape[1] % Q_TILE == 0
    c1 = 4 * width
    for layer in range(depth):
        lam_init = _lambda_init(layer)
        w = w_in[layer].astype(_BF16)
        wc = w[:, :c1]
        wq, wk, wv, wzb = (w[:, c1 + i * d:c1 + (i + 1) * d] for i in range(4))
        wt = jnp.concatenate([wq.T, wv.T, wzb.T], axis=0)
        wg = w[:, c1 + 4 * d:]
        qt, k, vt, zbt, gb, ma = _proj_call(
            x, norm_w[layer][None, :], wc, wk, wt, wg, b_gate[layer][None, :],
            conv_w[layer], w_out_a[layer].astype(_BF16))
        lam_params = jnp.stack([lambda_q1[layer], lambda_k1[layer],
                                lambda_q2[layer], lambda_k2[layer]], axis=0)
        hnw_b = jnp.broadcast_to(head_norm_w[layer][:, None], (DA_VALUE_DIM, Q_TILE))
        ot = _attn_call(lam_params, hnw_b, qt, k, vt, zbt, lam_init)
        x = _merge_call(x, ot, gb, ma, w_out_b[layer].astype(_BF16),
                        w_o[layer].astype(_BF16), final_norm_w[None, :])
    return x
```
